```python
import jax
import jax.numpy as jnp
from jax import lax
import numpy as np

D_MODEL = 1024
BATCH = 32
SEQ = 256
DEPTH = 2
DEC_BATCH = 8
DEC_SEQ = 4096
PAST_LEN = 256

GRID_W = 64
MIX_W = D_MODEL
MLSTM_W = D_MODEL // 4
MLSTM_HEADS = 4
MLSTM_DH = MLSTM_W // MLSTM_HEADS
N_GATES = 4 * MLSTM_HEADS
CHUNK = 64
CONV_CH = D_MODEL // 4
CONV_WIDTH = 31
CONV_PAD = CONV_WIDTH // 2
MLA_W = D_MODEL // 2
MLA_HEADS = 8
MLA_V = MLA_W // MLA_HEADS
MLA_NOPE = 64
MLA_ROPE = 32
MLA_QK = MLA_NOPE + MLA_ROPE
Q_RANK = 256
KV_RANK = 128
Q_BLOCK = 128
ROPE_BASE = 10000.0
D_FF = (8 * D_MODEL + 3 * 256 - 1) // (3 * 256) * 256
EPS = 1e-6
IN_SIZES = (MLSTM_W, MLSTM_W, MLSTM_W, MLSTM_W, N_GATES, 2 * CONV_CH, Q_RANK, KV_RANK, MLA_ROPE)
IN_COLS = 4 * MLSTM_W + N_GATES + 2 * CONV_CH + Q_RANK + KV_RANK + MLA_ROPE

kernel_name = 'hybrid_mlstm_conv_mla_dit_step'


def rmsnorm(x, g):
    x32 = x.astype(jnp.float32)
    y = x32 * lax.rsqrt(jnp.mean(x32 * x32, axis=-1, keepdims=True) + EPS)
    return (y * g.astype(jnp.float32)).astype(x.dtype)


def layernorm(x, g, b):
    x32 = x.astype(jnp.float32)
    mu = jnp.mean(x32, axis=-1, keepdims=True)
    xc = x32 - mu
    y = xc * lax.rsqrt(jnp.mean(xc * xc, axis=-1, keepdims=True) + EPS)
    return (y * g.astype(jnp.float32) + b.astype(jnp.float32)).astype(x.dtype)


def split_in(z):
    out, start = [], 0
    for size in IN_SIZES:
        out.append(z[..., start:start + size])
        start += size
    return out


def mlstm_scan(q, k, v, log_i, log_f, C0, n0, m0):
    B, T, H, DH = q.shape
    NC = T // CHUNK
    f32 = jnp.float32

    def chunks(a):
        a = a.astype(f32).reshape((B, NC, CHUNK) + a.shape[2:])
        return jnp.moveaxis(jnp.moveaxis(a, 1, 0), 3, 2)

    tril = jnp.tril(jnp.ones((CHUNK, CHUNK), dtype=bool))

    def step(carry, inp):
        C, n, m = carry
        qc, kc, vc, ic, fc = inp
        b = jnp.cumsum(fc, axis=-1)
        d = jnp.where(tril, b[..., :, None] - b[..., None, :] + ic[..., None, :], -jnp.inf)
        inter = b + m[..., None]
        m_row = jnp.maximum(inter, jnp.max(d, axis=-1))
        w_inter = jnp.exp(inter - m_row)
        s = jnp.einsum('bhld,bhsd->bhls', qc, kc) * jnp.exp(d - m_row[..., None])
        num = w_inter[..., None] * jnp.einsum('bhld,bhde->bhle', qc, C) + jnp.einsum('bhls,bhse->bhle', s, vc)
        den = w_inter * jnp.einsum('bhld,bhd->bhl', qc, n) + jnp.sum(s, axis=-1)
        h = num / jnp.maximum(jnp.abs(den), jnp.exp(-m_row))[..., None]
        b_end = b[..., -1]
        g = b_end[..., None] - b + ic
        m_new = jnp.maximum(b_end + m, jnp.max(g, axis=-1))
        decay = jnp.exp(b_end + m - m_new)
        kw = kc * jnp.exp(g - m_new[..., None])[..., None]
        C_new = decay[..., None, None] * C + jnp.einsum('bhld,bhle->bhde', kw, vc)
        n_new = decay[..., None] * n + jnp.sum(kw, axis=2)
        return (C_new, n_new, m_new), h

    init = (C0.astype(f32), n0.astype(f32), m0.astype(f32))
    final, h = lax.scan(step, init, (chunks(q), chunks(k), chunks(v), chunks(log_i), chunks(log_f)))
    h = jnp.moveaxis(jnp.moveaxis(h, 2, 3), 0, 1).reshape(B, T, H, DH)
    return h, final


def mlstm_mixer(q_raw, k_raw, v_raw, o_raw, gate_raw, b_gate, norm_g, init_fwd, init_bwd):
    B, T, _ = q_raw.shape
    shp = (B, T, MLSTM_HEADS, MLSTM_DH)
    q = q_raw.reshape(shp) * (MLSTM_DH ** -0.5)
    k = k_raw.reshape(shp)
    v = v_raw.reshape(shp)
    g = (gate_raw + b_gate).astype(jnp.float32).reshape(B, T, 4, MLSTM_HEADS)
    li_f, lf_f = g[:, :, 0], jax.nn.log_sigmoid(g[:, :, 1])
    li_b, lf_b = g[:, :, 2], jax.nn.log_sigmoid(g[:, :, 3])
    h_f, st_f = mlstm_scan(q, k, v, li_f, lf_f, *init_fwd)
    rev = lambda a: jnp.flip(a, axis=1)
    h_b, st_b = mlstm_scan(rev(q), rev(k), rev(v), rev(li_b), rev(lf_b), *init_bwd)
    h = (h_f + rev(h_b)).astype(q_raw.dtype)
    h = rmsnorm(h, norm_g.reshape(MLSTM_HEADS, MLSTM_DH)).reshape(B, T, MLSTM_W)
    return h * jax.nn.sigmoid(o_raw), st_f, st_b


def conv_module(a, dw_w, dw_b, ln_g, ln_b, pw_w):
    u = a[..., :CONV_CH] * jax.nn.sigmoid(a[..., CONV_CH:])
    y = lax.conv_general_dilated(u, dw_w[:, None, :], window_strides=(1,), padding=[(CONV_PAD, CONV_PAD)],
                                 dimension_numbers=('NWC', 'WIO', 'NWC'), feature_group_count=CONV_CH) + dw_b
    return jax.nn.silu(layernorm(y, ln_g, ln_b)) @ pw_w


def axial_rope(a):
    T = a.shape[1]
    ROWS = T // GRID_W
    rows = jnp.repeat(jnp.arange(ROWS), GRID_W)
    cols = jnp.tile(jnp.arange(GRID_W), ROWS)
    half = MLA_ROPE // 2
    quarter = half // 2
    freqs = ROPE_BASE ** (-jnp.arange(quarter, dtype=jnp.float32) / quarter)

    def rot(x, pos):
        ang = pos.astype(jnp.float32)[:, None] * freqs[None, :]
        cos = jnp.cos(ang)[None, :, None, :].astype(x.dtype)
        sin = jnp.sin(ang)[None, :, None, :].astype(x.dtype)
        x1, x2 = x[..., :quarter], x[..., quarter:]
        return jnp.concatenate([x1 * cos - x2 * sin, x1 * sin + x2 * cos], axis=-1)

    return jnp.concatenate([rot(a[..., :half], rows), rot(a[..., half:], cols)], axis=-1)


def with_rope(a):
    return jnp.concatenate([a[..., :MLA_NOPE], axial_rope(a[..., MLA_NOPE:])], axis=-1)


def mla_queries(cq_raw, cq_g, w_uq, qn_g):
    B, T, _ = cq_raw.shape
    q = (rmsnorm(cq_raw, cq_g) @ w_uq).reshape(B, T, MLA_HEADS, MLA_QK)
    return rmsnorm(q, qn_g)


def mla_keys_values(ckv, krope, w_ukv, kn_g):
    B, T, _ = ckv.shape
    kv = (ckv @ w_ukv).reshape(B, T, MLA_HEADS, MLA_NOPE + MLA_V)
    k_nope, v = kv[..., :MLA_NOPE], kv[..., MLA_NOPE:]
    k_r = jnp.broadcast_to(krope[:, :, None, :], (B, T, MLA_HEADS, MLA_ROPE))
    k = rmsnorm(jnp.concatenate([k_nope, k_r], axis=-1), kn_g)
    return k, v


def attend(q, k, v):
    s = jnp.einsum('bqhd,bkhd->bhqk', q, k).astype(jnp.float32) * (MLA_QK ** -0.5)
    p = jax.nn.softmax(s, axis=-1).astype(v.dtype)
    return jnp.einsum('bhqk,bkhd->bqhd', p, v)


def blocked_attend(q, k, v):
    B, T, H, DQ = q.shape
    qb = jnp.moveaxis(q.reshape(B, T // Q_BLOCK, Q_BLOCK, H, DQ), 1, 0)
    out = lax.map(lambda qq: attend(qq, k, v), qb)
    return jnp.moveaxis(out, 0, 1).reshape(B, T, H, MLA_V)


def trunk_layer(x, cond, p, ctx):
    sh1, sc1, g1, sh2, sc2, g2 = jnp.split(jax.nn.silu(cond) @ p['w_ada'] + p['b_ada'], 6, axis=-1)
    h = rmsnorm(x, p['norm1_g']) * (1 + sc1) + sh1
    q_m, k_m, v_m, o_m, gate_m, conv_in, cq_raw, ckv_raw, krope = split_in(h @ p['w_in'])
    B, T, _ = x.shape
    if ctx is None:
        zero_state = (jnp.zeros((B, MLSTM_HEADS, MLSTM_DH, MLSTM_DH), jnp.float32),
                      jnp.zeros((B, MLSTM_HEADS, MLSTM_DH), jnp.float32),
                      jnp.zeros((B, MLSTM_HEADS), jnp.float32))
        init_f = zero_state
        init_b = zero_state
    else:
        init_f, init_b = ctx[2], ctx[3]
    y_m, st_f, st_b = mlstm_mixer(q_m, k_m, v_m, o_m, gate_m, p['b_gates'], p['mlstm_norm_g'], init_f, init_b)
    y_c = conv_module(conv_in, p['conv_dw_w'], p['conv_dw_b'], p['conv_ln_g'], p['conv_ln_b'], p['conv_pw_w'])
    q = mla_queries(cq_raw, p['mla_cq_g'], p['mla_w_uq'], p['mla_qn_g'])
    ckv = rmsnorm(ckv_raw, p['mla_ckv_g'])
    k, v = mla_keys_values(ckv, krope, p['mla_w_ukv'], p['mla_kn_g'])
    if ctx is None:
        y_a = attend(q, k, v)
    else:
        k_ctx, v_ctx = mla_keys_values(ctx[0], ctx[1], p['mla_w_ukv'], p['mla_kn_g'])
        y_a = blocked_attend(with_rope(q), jnp.concatenate([with_rope(k), k_ctx], axis=1),
                             jnp.concatenate([v, v_ctx], axis=1))
    mix = jnp.concatenate([y_m, y_c, y_a.reshape(B, T, MLA_W)], axis=-1) @ p['w_out']
    x = x + g1 * mix
    h2 = rmsnorm(x, p['norm2_g']) * (1 + sc2) + sh2
    gu = h2 @ p['ffn_w_gu']
    x = x + g2 * ((jax.nn.silu(gu[..., :D_FF]) * gu[..., D_FF:]) @ p['ffn_w_down'])
    return x, (ckv, krope, st_f, st_b)


def setup_inputs(seed: int = 0) -> dict:
    key = jax.random.key(seed)
    ks = jax.random.split(key, 32)
    f32 = jnp.float32
    nrm = lambda k, shape, scale: jax.random.normal(k, shape, f32) * scale
    gain = lambda k, shape: 1.0 + 0.02 * jax.random.normal(k, shape, f32)
    forget_base = jnp.linspace(3.0, 6.0, MLSTM_HEADS)
    zeros_h = jnp.zeros((MLSTM_HEADS,), f32)
    gate_base = jnp.stack([zeros_h, forget_base, zeros_h, forget_base]).reshape(N_GATES)
    return {
        'x_prompt': nrm(ks[0], (BATCH, SEQ, D_MODEL), 1.0),
        'x_sample': nrm(ks[1], (DEC_BATCH, DEC_SEQ, D_MODEL), 1.0),
        'cache_ckv': nrm(ks[2], (DEC_BATCH, DEPTH, PAST_LEN, KV_RANK), 1.0),
        'cache_krope': nrm(ks[3], (DEC_BATCH, DEPTH, PAST_LEN, MLA_ROPE), 1.0),
        'state_C': nrm(ks[4], (DEC_BATCH, DEPTH, 2, MLSTM_HEADS, MLSTM_DH, MLSTM_DH), 0.5),
        'state_n': nrm(ks[5], (DEC_BATCH, DEPTH, 2, MLSTM_HEADS, MLSTM_DH), 0.5),
        'state_m': nrm(ks[6], (DEC_BATCH, DEPTH, 2, MLSTM_HEADS), 1.0),
        'c': nrm(ks[7], (DEC_BATCH, D_MODEL), 1.0),
        'c_ctx': nrm(ks[8], (D_MODEL,), 1.0),
        'w_ada': nrm(ks[9], (DEPTH, D_MODEL, 6 * D_MODEL), D_MODEL ** -0.5),
        'b_ada': nrm(ks[10], (DEPTH, 6 * D_MODEL), 0.01),
        'norm1_g': gain(ks[11], (DEPTH, D_MODEL)),
        'w_in': nrm(ks[12], (DEPTH, D_MODEL, IN_COLS), D_MODEL ** -0.5),
        'b_gates': gate_base[None, :] + nrm(ks[13], (DEPTH, N_GATES), 0.1),
        'mlstm_norm_g': gain(ks[14], (DEPTH, MLSTM_W)),
        'conv_dw_w': nrm(ks[15], (DEPTH, CONV_WIDTH, CONV_CH), CONV_WIDTH ** -0.5),
        'conv_dw_b': nrm(ks[16], (DEPTH, CONV_CH), 0.01),
        'conv_ln_g': gain(ks[17], (DEPTH, CONV_CH)),
        'conv_ln_b': nrm(ks[18], (DEPTH, CONV_CH), 0.01),
        'conv_pw_w': nrm(ks[19], (DEPTH, CONV_CH, CONV_CH), CONV_CH ** -0.5),
        'mla_cq_g': gain(ks[20], (DEPTH, Q_RANK)),
        'mla_w_uq': nrm(ks[21], (DEPTH, Q_RANK, MLA_HEADS * MLA_QK), Q_RANK ** -0.5),
        'mla_ckv_g': gain(ks[22], (DEPTH, KV_RANK)),
        'mla_w_ukv': nrm(ks[23], (DEPTH, KV_RANK, MLA_HEADS * (MLA_NOPE + MLA_V)), KV_RANK ** -0.5),
        'mla_qn_g': gain(ks[24], (DEPTH, MLA_QK)),
        'mla_kn_g': gain(ks[25], (DEPTH, MLA_QK)),
        'w_out': nrm(ks[26], (DEPTH, MIX_W, D_MODEL), MIX_W ** -0.5),
        'norm2_g': gain(ks[27], (DEPTH, D_MODEL)),
        'ffn_w_gu': nrm(ks[28], (DEPTH, D_MODEL, 2 * D_FF), D_MODEL ** -0.5),
        'ffn_w_down': nrm(ks[29], (DEPTH, D_FF, D_MODEL), D_FF ** -0.5),
    }


def reference(x_prompt, x_sample, cache_ckv, cache_krope, state_C, state_n, state_m, c, c_ctx,
              w_ada, b_ada, norm1_g, w_in, b_gates, mlstm_norm_g, conv_dw_w, conv_dw_b, conv_ln_g,
              conv_ln_b, conv_pw_w, mla_cq_g, mla_w_uq, mla_ckv_g, mla_w_ukv, mla_qn_g, mla_kn_g,
              w_out, norm2_g, ffn_w_gu, ffn_w_down):
    def params(l):
        return {'w_ada': w_ada[l], 'b_ada': b_ada[l], 'norm1_g': norm1_g[l], 'w_in': w_in[l],
                'b_gates': b_gates[l], 'mlstm_norm_g': mlstm_norm_g[l], 'conv_dw_w': conv_dw_w[l],
                'conv_dw_b': conv_dw_b[l], 'conv_ln_g': conv_ln_g[l], 'conv_ln_b': conv_ln_b[l],
                'conv_pw_w': conv_pw_w[l], 'mla_cq_g': mla_cq_g[l], 'mla_w_uq': mla_w_uq[l],
                'mla_ckv_g': mla_ckv_g[l], 'mla_w_ukv': mla_w_ukv[l], 'mla_qn_g': mla_qn_g[l],
                'mla_kn_g': mla_kn_g[l], 'w_out': w_out[l], 'norm2_g': norm2_g[l],
                'ffn_w_gu': ffn_w_gu[l], 'ffn_w_down': ffn_w_down[l]}

    y_prompt = x_prompt
    cond_ctx = c_ctx[None, None, :]
    ckv_l, kr_l, C_l, n_l, m_l = [], [], [], [], []
    for l in range(DEPTH):
        y_prompt, (ckv, kr, st_f, st_b) = trunk_layer(y_prompt, cond_ctx, params(l), None)
        ckv_l.append(ckv)
        kr_l.append(kr)
        C_l.append(jnp.stack([st_f[0], st_b[0]], axis=1))
        n_l.append(jnp.stack([st_f[1], st_b[1]], axis=1))
        m_l.append(jnp.stack([st_f[2], st_b[2]], axis=1))
    new_cache_ckv = jnp.stack(ckv_l, axis=1)
    new_cache_krope = jnp.stack(kr_l, axis=1)
    new_state_C = jnp.stack(C_l, axis=1)
    new_state_n = jnp.stack(n_l, axis=1)
    new_state_m = jnp.stack(m_l, axis=1)

    y_sample = x_sample
    cond_lat = c[:, None, :]
    for l in range(DEPTH):
        ctx = (cache_ckv[:, l], cache_krope[:, l],
               (state_C[:, l, 0], state_n[:, l, 0], state_m[:, l, 0]),
               (state_C[:, l, 1], state_n[:, l, 1], state_m[:, l, 1]))
        y_sample, _ = trunk_layer(y_sample, cond_lat, params(l), ctx)

    return (y_prompt, y_sample, new_cache_ckv, new_cache_krope, new_state_C, new_state_n, new_state_m)
```

```python
import functools

import jax
import jax.numpy as jnp
from jax import lax
from jax.experimental import pallas as pl
from jax.experimental.pallas import tpu as pltpu

F32 = jnp.float32
BF16 = jnp.bfloat16
EPS = 1e-6
GRID_W = 64
ROPE_BASE = 10000.0

LANES = 128
MLSTM_HEADS = 4
MLSTM_DH = 64
MLSTM_CHUNK = 128
CONV_WIDTH = 31
CONV_PAD = CONV_WIDTH // 2
CONV_HALO = 16
MLA_HEADS = 8
MLA_NOPE = 64
MLA_ROPE = 32
MLA_QK = MLA_NOPE + MLA_ROPE
MLA_V = 64
N_GATES = 4 * MLSTM_HEADS
VMEM_LIMIT = 56 * 1024 * 1024


def _params(*sem):
    return pltpu.CompilerParams(dimension_semantics=sem, vmem_limit_bytes=VMEM_LIMIT)


def _dot(a, b):
    return jnp.dot(a, b, preferred_element_type=F32)


def _sigmoid(x):
    return jax.nn.sigmoid(x)


def _ada_kernel(c_ref, w_ref, b_ref, o_ref):
    c = c_ref[...]
    s = (c * _sigmoid(c)).astype(BF16)
    o_ref[0] = _dot(s, w_ref[0].astype(BF16)) + b_ref[0]


def _ada(conds, w_ada, b_ada):
    depth, d, n6 = w_ada.shape
    rows = conds.shape[0]
    tn = 768
    return pl.pallas_call(
        _ada_kernel,
        grid=(depth, n6 // tn),
        in_specs=[
            pl.BlockSpec((rows, d), lambda l, j: (0, 0)),
            pl.BlockSpec((1, d, tn), lambda l, j: (l, 0, j)),
            pl.BlockSpec((1, 1, tn), lambda l, j: (l, 0, j)),
        ],
        out_specs=pl.BlockSpec((1, rows, tn), lambda l, j: (l, 0, j)),
        out_shape=jax.ShapeDtypeStruct((depth, rows, n6), F32),
        compiler_params=_params("arbitrary", "arbitrary"),
        name="ada",
    )(conds, w_ada, b_ada.reshape(depth, 1, n6))


_C_Q, _C_K, _C_V, _C_O, _C_A1, _C_A2, _C_CQ = 0, 256, 512, 768, 1024, 1280, 1536
_C_CKV, _C_KR, _C_KRP, _C_G, _C_END = 1792, 1920, 2048, 2176, 2304


def _inproj_kernel(x_ref, mod_ref, g_ref, w_ref, bg_ref,
                   q_ref, kt_ref, v_ref, o_ref, u_ref, cq_ref, ckv_ref, kr_ref, krp_ref,
                   gc_ref, gr_ref):
    x = x_ref[...]
    ms = jnp.mean(x * x, axis=-1, keepdims=True)
    xn = x * lax.rsqrt(ms + EPS) * g_ref[...]
    sh = mod_ref[0, 0:1, :]
    sc = mod_ref[0, 1:2, :]
    h = (xn * (1.0 + sc) + sh).astype(BF16)

    def proj(a, b):
        return _dot(h, w_ref[:, a:b])

    q_ref[...] = proj(_C_Q, _C_K).astype(BF16)
    zk = proj(_C_K, _C_V)
    for c in range(zk.shape[0] // MLSTM_CHUNK):
        kt_ref[c] = zk[c * MLSTM_CHUNK:(c + 1) * MLSTM_CHUNK, :].T.astype(BF16)
    v_ref[...] = proj(_C_V, _C_O).astype(BF16)
    o_ref[...] = proj(_C_O, _C_A1)
    a1 = proj(_C_A1, _C_A2)
    a2 = proj(_C_A2, _C_CQ)
    u_ref[...] = a1 * _sigmoid(a2)
    cq_ref[...] = proj(_C_CQ, _C_CKV)
    ckv_ref[...] = proj(_C_CKV, _C_KR)
    kr_ref[...] = proj(_C_KR, _C_KRP)
    krp_ref[...] = proj(_C_KRP, _C_G)
    g = proj(_C_G, _C_END) + bg_ref[...]
    lane = lax.broadcasted_iota(jnp.int32, g.shape, 1)
    is_forget = ((lane >= 4) & (lane < 8)) | ((lane >= 12) & (lane < 16))
    log_sig = jnp.minimum(g, 0.0) - jnp.log1p(jnp.exp(-jnp.abs(g)))
    g = jnp.where(is_forget, log_sig, g)
    gc_ref[...] = g
    for c in range(g.shape[0] // MLSTM_CHUNK):
        gr_ref[c] = g[c * MLSTM_CHUNK:(c + 1) * MLSTM_CHUNK, :].T[:N_GATES, :]


def _inproj(x, mod, g1, w, bg, *, seq, tm, mod_base, mod_stride):
    n, d = x.shape
    tpb = seq // tm
    tok = lambda c: pl.BlockSpec((tm, c), lambda i: (i, 0))
    out_shapes = [
        jax.ShapeDtypeStruct((n, 256), BF16),
        jax.ShapeDtypeStruct((n // MLSTM_CHUNK, 256, MLSTM_CHUNK), BF16),
        jax.ShapeDtypeStruct((n, 256), BF16),
        jax.ShapeDtypeStruct((n, 256), F32),
        jax.ShapeDtypeStruct((n, 256), F32),
        jax.ShapeDtypeStruct((n, 256), F32),
        jax.ShapeDtypeStruct((n, LANES), F32),
        jax.ShapeDtypeStruct((n, LANES), F32),
        jax.ShapeDtypeStruct((n, LANES), F32),
        jax.ShapeDtypeStruct((n, LANES), F32),
        jax.ShapeDtypeStruct((n // MLSTM_CHUNK, N_GATES, MLSTM_CHUNK), F32),
    ]
    cpt = tm // MLSTM_CHUNK
    out_specs = [tok(256), pl.BlockSpec((cpt, 256, MLSTM_CHUNK), lambda i: (i, 0, 0)), tok(256),
                 tok(256), tok(256), tok(256), tok(LANES), tok(LANES), tok(LANES), tok(LANES),
                 pl.BlockSpec((cpt, N_GATES, MLSTM_CHUNK), lambda i: (i, 0, 0))]
    return pl.pallas_call(
        _inproj_kernel,
        grid=(n // tm,),
        in_specs=[
            tok(d),
            pl.BlockSpec((1, 6, d), lambda i: (mod_base + mod_stride * (i // tpb), 0, 0)),
            pl.BlockSpec((1, d), lambda i: (0, 0)),
            pl.BlockSpec((d, _C_END), lambda i: (0, 0)),
            pl.BlockSpec((1, LANES), lambda i: (0, 0)),
        ],
        out_specs=out_specs,
        out_shape=out_shapes,
        compiler_params=_params("arbitrary"),
        name="inproj",
    )(x, mod, g1, w, bg)


def _split3(x):
    x1 = x.astype(BF16)
    r1 = x - x1.astype(F32)
    x2 = r1.astype(BF16)
    r2 = r1 - x2.astype(F32)
    return x1, x2, r2.astype(BF16)


def _mlstm_kernel(*refs, seq, has_init, write_final):
    q_ref, kt_ref, v_ref, o_ref, gc_ref, gr_ref, ng_ref = refs[:7]
    pos = 7
    if has_init:
        c0_ref, m0_ref = refs[pos:pos + 2]
        pos += 2
    y_ref = refs[pos]
    pos += 1
    if write_final:
        cf_ref, mf_ref = refs[pos:pos + 2]
        pos += 2
    c_scr = refs[pos]

    L = MLSTM_CHUNK
    nchunk = seq // L
    n_state = 2 * MLSTM_HEADS

    row = lax.broadcasted_iota(jnp.int32, (L, L), 0)
    col = lax.broadcasted_iota(jnp.int32, (L, L), 1)
    lower = row >= col
    upper = col >= row
    tri_low = lower.astype(BF16)
    tri_up = upper.astype(BF16)
    lane_lo = col < MLSTM_DH
    sub_lo = row < MLSTM_DH
    one_at_64 = (col == MLSTM_DH).astype(BF16)
    one_at_0 = (col == 0).astype(BF16)

    if has_init:
        c_scr[...] = c0_ref[0]
        m_init = tuple(m0_ref[0, i:i + 1, 0:1] for i in range(n_state))
    else:
        c_scr[...] = jnp.zeros(c_scr.shape, F32)
        m_init = tuple(jnp.zeros((1, 1), F32) for _ in range(n_state))

    def cumsum_col(g):
        x1, x2, x3 = _split3(g)
        return _dot(tri_low, x1) + _dot(tri_low, x2) + _dot(tri_low, x3)

    def cumsum_row(g):
        x1, x2, x3 = _split3(g)
        return _dot(x1, tri_up) + _dot(x2, tri_up) + _dot(x3, tri_up)

    def chunk(j, ms, finalize):
        new_ms = [None] * n_state
        for dirn in range(2):
            c = j if dirn == 0 else nchunk - 1 - j
            r0 = pl.multiple_of(c * L, L)
            gcol = gc_ref[pl.ds(r0, L), :]
            grow = gr_ref[c]
            bcol_all = cumsum_col(gcol)
            brow_all = cumsum_row(grow)
            if dirn == 1:
                bcol_all = bcol_all[L - 1:L, :] - bcol_all + gcol
                brow_all = brow_all[:, L - 1:L] - brow_all + grow
            mask = lower if dirn == 0 else upper
            for hp in range(MLSTM_HEADS // 2):
                lanes = slice(hp * LANES, (hp + 1) * LANES)
                qp = q_ref[pl.ds(r0, L), lanes]
                ktp = kt_ref[c, lanes, :]
                vp = v_ref[pl.ds(r0, L), lanes]
                hvals = []
                for hh in range(2):
                    head = hp * 2 + hh
                    idx = dirn * MLSTM_HEADS + head
                    gi = dirn * 2 * MLSTM_HEADS + head
                    gf = gi + MLSTM_HEADS
                    bcol = bcol_all[:, gf:gf + 1]
                    brow = brow_all[gf:gf + 1, :]
                    irow = grow[gi:gi + 1, :]
                    m = ms[idx]
                    rr = irow - brow
                    d = jnp.where(mask, bcol + rr, -jnp.inf)
                    inter = bcol + m
                    m_row = jnp.maximum(inter, jnp.max(d, axis=1, keepdims=True))
                    w_inter = jnp.exp(inter - m_row)
                    wmat = jnp.exp(d - m_row)
                    kt_h = jnp.where(sub_lo if hh == 0 else ~sub_lo, ktp, jnp.zeros_like(ktp))
                    s = _dot(qp, kt_h)
                    p = (s * wmat).astype(BF16)
                    if hh == 0:
                        vext = jnp.where(lane_lo, vp, one_at_64)
                    else:
                        vext = jnp.where(lane_lo, one_at_0, vp)
                    cst = c_scr[idx]
                    r = w_inter * _dot(qp, cst.astype(BF16)) + _dot(p, vext)
                    den_lane = MLSTM_DH if hh == 0 else 0
                    den = r[:, den_lane:den_lane + 1]
                    hvals.append(r / jnp.maximum(jnp.abs(den), jnp.exp(-m_row)))
                    b_end = brow[:, L - 1:L] if dirn == 0 else brow[:, 0:1]
                    g_row = b_end + rr
                    m_new = jnp.maximum(b_end + m, jnp.max(g_row, axis=1, keepdims=True))
                    decay = jnp.exp(b_end + m - m_new)
                    kw = (kt_h.astype(F32) * jnp.exp(g_row - m_new)).astype(BF16)
                    c_scr[idx] = decay * cst + _dot(kw, vext)
                    new_ms[idx] = m_new
                hpair = jnp.where(lane_lo, hvals[0], hvals[1])
                if finalize:
                    hs = hpair + y_ref[pl.ds(r0, L), lanes]
                    sq = hs * hs
                    ms0 = jnp.sum(jnp.where(lane_lo, sq, 0.0), axis=1, keepdims=True)
                    ms1 = jnp.sum(jnp.where(lane_lo, 0.0, sq), axis=1, keepdims=True)
                    rs = jnp.where(lane_lo, lax.rsqrt(ms0 / MLSTM_DH + EPS),
                                   lax.rsqrt(ms1 / MLSTM_DH + EPS))
                    gate = _sigmoid(o_ref[pl.ds(r0, L), lanes])
                    y_ref[pl.ds(r0, L), lanes] = hs * rs * ng_ref[:, lanes] * gate
                else:
                    y_ref[pl.ds(r0, L), lanes] = hpair
        return tuple(new_ms)

    half = nchunk // 2
    ms = lax.fori_loop(0, half, lambda j, ms: chunk(j, ms, False), m_init)
    ms = lax.fori_loop(half, nchunk, lambda j, ms: chunk(j, ms, True), ms)

    if write_final:
        cf_ref[0] = c_scr[...]
        for i in range(n_state):
            mf_ref[0, i:i + 1, :] = jnp.broadcast_to(ms[i], (1, LANES))


def _mlstm(q, kt, v, o, gc, gr, ng, c0, m0, *, batch, seq, write_final):
    assert (seq // MLSTM_CHUNK) % 2 == 0
    has_init = c0 is not None
    n_state = 2 * MLSTM_HEADS
    tok = lambda c: pl.BlockSpec((seq, c), lambda b: (b, 0))
    nchunk = seq // MLSTM_CHUNK
    in_specs = [tok(256), pl.BlockSpec((nchunk, 256, MLSTM_CHUNK), lambda b: (b, 0, 0)), tok(256),
                tok(256), tok(LANES),
                pl.BlockSpec((nchunk, N_GATES, MLSTM_CHUNK), lambda b: (b, 0, 0)),
                pl.BlockSpec((1, 256), lambda b: (0, 0))]
    args = [q, kt, v, o, gc, gr, ng]
    if has_init:
        in_specs += [pl.BlockSpec((1, n_state, LANES, LANES), lambda b: (b, 0, 0, 0)),
                     pl.BlockSpec((1, n_state, LANES), lambda b: (b, 0, 0))]
        args += [c0, m0]
    out_specs = [tok(256)]
    out_shape = [jax.ShapeDtypeStruct((batch * seq, 256), F32)]
    if write_final:
        out_specs += [pl.BlockSpec((1, n_state, LANES, LANES), lambda b: (b, 0, 0, 0)),
                      pl.BlockSpec((1, n_state, LANES), lambda b: (b, 0, 0))]
        out_shape += [jax.ShapeDtypeStruct((batch, n_state, LANES, LANES), F32),
                      jax.ShapeDtypeStruct((batch, n_state, LANES), F32)]
    return pl.pallas_call(
        functools.partial(_mlstm_kernel, seq=seq, has_init=has_init, write_final=write_final),
        grid=(batch,),
        in_specs=in_specs,
        out_specs=out_specs,
        out_shape=out_shape,
        scratch_shapes=[pltpu.VMEM((n_state, LANES, LANES), F32)],
        compiler_params=_params("arbitrary"),
        name="mlstm",
    )(*args)


def _conv_kernel(u_ref, dw_ref, db_ref, lg_ref, lb_ref, pw_ref, o_ref, pad_ref, *, seq, rows):
    ch = u_ref.shape[1]
    zeros = jnp.zeros((CONV_HALO, ch), F32)
    pad_ref[0:CONV_HALO, :] = zeros
    pad_ref[seq + CONV_HALO:seq + 2 * CONV_HALO, :] = zeros
    pad_ref[CONV_HALO:seq + CONV_HALO, :] = u_ref[...]
    first = CONV_HALO - CONV_PAD

    def tile(i, carry):
        t0 = pl.multiple_of(i * rows, rows)
        win = pad_ref[pl.ds(t0, rows + 2 * CONV_HALO), :]
        acc = jnp.broadcast_to(db_ref[...], (rows, ch))
        for j in range(CONV_WIDTH):
            acc = acc + win[first + j:first + j + rows, :] * dw_ref[j:j + 1, :]
        mu = jnp.mean(acc, axis=-1, keepdims=True)
        xc = acc - mu
        var = jnp.mean(xc * xc, axis=-1, keepdims=True)
        y = xc * lax.rsqrt(var + EPS) * lg_ref[...] + lb_ref[...]
        s = (y * _sigmoid(y)).astype(BF16)
        o_ref[pl.ds(t0, rows), :] = _dot(s, pw_ref[...]).astype(BF16)
        return carry

    lax.fori_loop(0, seq // rows, tile, 0)


def _conv(u, dw, db, lg, lb, pw, *, batch, seq):
    ch = u.shape[1]
    rows = 128
    full = lambda r, c: pl.BlockSpec((r, c), lambda b: (0, 0))
    return pl.pallas_call(
        functools.partial(_conv_kernel, seq=seq, rows=rows),
        grid=(batch,),
        in_specs=[pl.BlockSpec((seq, ch), lambda b: (b, 0)), full(CONV_WIDTH, ch), full(1, ch),
                  full(1, ch), full(1, ch), full(ch, ch)],
        out_specs=pl.BlockSpec((seq, ch), lambda b: (b, 0)),
        out_shape=jax.ShapeDtypeStruct((batch * seq, ch), BF16),
        scratch_shapes=[pltpu.VMEM((seq + 2 * CONV_HALO, ch), F32)],
        compiler_params=_params("arbitrary"),
        name="conv",
    )(u, dw, db, lg, lb, pw)


def _mlaprep_kernel(*refs, do_q, norm_ckv, rope):
    refs = list(refs)
    take = lambda n: [refs.pop(0) for _ in range(n)]
    ckv_ref, kr_ref = take(2)
    krp_ref = take(1)[0] if rope else None
    cq_ref = take(1)[0] if do_q else None
    cos_ref, sin_ref = take(2) if rope else (None, None)
    ckvg_ref, wk_ref, wv_ref, gk_ref = take(4)
    gkp_ref = take(1)[0] if rope else None
    if do_q:
        cqg_ref, wq_ref, gq_ref = take(3)
        wqp_ref, gqp_ref = take(2) if rope else (None, None)
        q_ref = take(1)[0]
    k_ref, v_ref, ckvn_ref = take(3)

    ckv = ckv_ref[...]
    if norm_ckv:
        ms = jnp.mean(ckv * ckv, axis=-1, keepdims=True)
        ckv = ckv * lax.rsqrt(ms + EPS) * ckvg_ref[...]
    ckvn_ref[...] = ckv
    ckvb = ckv.astype(BF16)

    kr = kr_ref[...]
    kr_ss = jnp.sum(kr * kr, axis=-1, keepdims=True)
    gk = gk_ref[...]
    lane = lax.broadcasted_iota(jnp.int32, kr.shape, 1)
    one_at_64 = (lane == MLA_V).astype(F32)
    if rope:
        cos = cos_ref[...]
        sin = sin_ref[...]
        k_partner = krp_ref[...] * gkp_ref[...] * sin
    for h in range(MLA_HEADS):
        kn = _dot(ckvb, wk_ref[h])
        ss = jnp.sum(kn * kn, axis=-1, keepdims=True) + kr_ss
        r = lax.rsqrt(ss / MLA_QK + EPS)
        kf = (kn + kr) * gk
        if rope:
            kf = kf * cos + k_partner
        k_ref[0, h] = (kf * r).astype(BF16)
        v_ref[0, h] = (_dot(ckvb, wv_ref[h]) + one_at_64).astype(BF16)

    if do_q:
        cq = cq_ref[...]
        ms = jnp.mean(cq * cq, axis=-1, keepdims=True)
        cqn = (cq * lax.rsqrt(ms + EPS) * cqg_ref[...]).astype(BF16)
        gq = gq_ref[...]
        scale = MLA_QK ** -0.5
        for h in range(MLA_HEADS):
            qh = _dot(cqn, wq_ref[h])
            r = lax.rsqrt(jnp.sum(qh * qh, axis=-1, keepdims=True) / MLA_QK + EPS)
            qf = qh * gq
            if rope:
                qf = qf * cos + _dot(cqn, wqp_ref[h]) * gqp_ref[...] * sin
            q_ref[0, h] = (qf * (r * scale)).astype(BF16)


def _mlaprep(ckv, kr, krp, cq, cos, sin, w, *, batch, seq, tm, do_q, norm_ckv, rope):
    tpb = seq // tm
    tok = lambda c: pl.BlockSpec((tm, c), lambda b, t: (b * tpb + t, 0))
    pos = lambda: pl.BlockSpec((tm, LANES), lambda b, t: (t, 0))
    full2 = lambda a: pl.BlockSpec(a.shape, lambda b, t: (0, 0))
    full3 = lambda a: pl.BlockSpec(a.shape, lambda b, t: (0, 0, 0))
    heads = pl.BlockSpec((1, MLA_HEADS, tm, LANES), lambda b, t: (b, 0, t, 0))
    args, specs = [ckv, kr], [tok(LANES), tok(LANES)]
    if rope:
        args.append(krp); specs.append(tok(LANES))
    if do_q:
        args.append(cq); specs.append(tok(cq.shape[1]))
    if rope:
        args += [cos, sin]; specs += [pos(), pos()]
    args += [w['ckv_g'], w['wk'], w['wv'], w['gk']]
    specs += [full2(w['ckv_g']), full3(w['wk']), full3(w['wv']), full2(w['gk'])]
    if rope:
        args.append(w['gkp']); specs.append(full2(w['gkp']))
    if do_q:
        args += [w['cq_g'], w['wq'], w['gq']]
        specs += [full2(w['cq_g']), full3(w['wq']), full2(w['gq'])]
        if rope:
            args += [w['wqp'], w['gqp']]
            specs += [full3(w['wqp']), full2(w['gqp'])]
    hshape = jax.ShapeDtypeStruct((batch, MLA_HEADS, seq, LANES), BF16)
    out_shape, out_specs = [], []
    if do_q:
        out_shape.append(hshape); out_specs.append(heads)
    out_shape += [hshape, hshape, jax.ShapeDtypeStruct((batch * seq, LANES), F32)]
    out_specs += [heads, heads, tok(LANES)]
    return pl.pallas_call(
        functools.partial(_mlaprep_kernel, do_q=do_q, norm_ckv=norm_ckv, rope=rope),
        grid=(batch, tpb),
        in_specs=specs,
        out_specs=out_specs,
        out_shape=out_shape,
        compiler_params=_params("arbitrary", "arbitrary"),
        name="mlaprep",
    )(*args)


def _attn_kernel(*refs, tk, has_ctx):
    if has_ctx:
        q_ref, k_ref, v_ref, kc_ref, vc_ref, o_ref = refs
    else:
        q_ref, k_ref, v_ref, o_ref = refs
    tq = q_ref.shape[2]
    nk = k_ref.shape[2] // tk
    qs = [q_ref[0, hh] for hh in range(2)]

    def update(hh, kblk, vblk, m, acc):
        s = lax.dot_general(qs[hh], kblk, (((1,), (1,)), ((), ())), preferred_element_type=F32)
        m_new = jnp.maximum(m, jnp.max(s, axis=-1, keepdims=True))
        p = jnp.exp(s - m_new).astype(BF16)
        acc = jnp.exp(m - m_new) * acc + _dot(p, vblk)
        return m_new, acc

    carry = []
    for hh in range(2):
        m = jnp.full((tq, 1), -jnp.inf, F32)
        acc = jnp.zeros((tq, LANES), F32)
        if has_ctx:
            m, acc = update(hh, kc_ref[0, hh], vc_ref[0, hh], m, acc)
        carry += [m, acc]

    def body(c, carry):
        k0 = pl.multiple_of(c * tk, tk)
        out = []
        for hh in range(2):
            m, acc = update(hh, k_ref[0, hh, pl.ds(k0, tk), :], v_ref[0, hh, pl.ds(k0, tk), :],
                            carry[2 * hh], carry[2 * hh + 1])
            out += [m, acc]
        return tuple(out)

    carry = lax.fori_loop(0, nk, body, tuple(carry))
    outs = []
    for hh in range(2):
        acc = carry[2 * hh + 1]
        outs.append(acc / acc[:, MLA_V:MLA_V + 1])
    lane = lax.broadcasted_iota(jnp.int32, (tq, LANES), 1)
    o = jnp.where(lane < MLA_V, outs[0], pltpu.roll(outs[1], MLA_V, axis=1))
    o_ref[0] = o.astype(BF16)


def _attn(q, k, v, kc, vc, *, tq, tk):
    batch, heads, seq, _ = q.shape
    has_ctx = kc is not None
    qspec = pl.BlockSpec((1, 2, tq, LANES), lambda b, hp, i: (b, hp, i, 0))
    kvspec = lambda a: pl.BlockSpec((1, 2, a.shape[2], LANES), lambda b, hp, i: (b, hp, 0, 0))
    args, specs = [q, k, v], [qspec, kvspec(k), kvspec(v)]
    if has_ctx:
        args += [kc, vc]; specs += [kvspec(kc), kvspec(vc)]
    return pl.pallas_call(
        functools.partial(_attn_kernel, tk=tk, has_ctx=has_ctx),
        grid=(batch, heads // 2, seq // tq),
        in_specs=specs,
        out_specs=pl.BlockSpec((1, tq, LANES), lambda b, hp, i: (b, i, hp)),
        out_shape=jax.ShapeDtypeStruct((batch, seq, heads * MLA_V), BF16),
        compiler_params=_params("arbitrary", "arbitrary", "arbitrary"),
        name="attn",
    )(*args)


def _outffn_kernel(x_ref, ym_ref, yc_ref, ya_ref, mod_ref, g_ref, wom_ref, woc_ref, woa_ref,
                   wg_ref, wu_ref, wd_ref, o_ref, *, ck):
    mix = (_dot(ym_ref[...].astype(BF16), wom_ref[...]) + _dot(yc_ref[...], woc_ref[...])
           + _dot(ya_ref[...], woa_ref[...]))
    x1 = x_ref[...] + mod_ref[0, 2:3, :] * mix
    ms = jnp.mean(x1 * x1, axis=-1, keepdims=True)
    h2 = x1 * lax.rsqrt(ms + EPS) * g_ref[...]
    h2 = (h2 * (1.0 + mod_ref[0, 4:5, :]) + mod_ref[0, 3:4, :]).astype(BF16)
    dff = wg_ref.shape[1]
    acc = jnp.zeros(x1.shape, F32)
    for c in range(dff // ck):
        cols = slice(c * ck, (c + 1) * ck)
        g = _dot(h2, wg_ref[:, cols])
        u = _dot(h2, wu_ref[:, cols])
        a = (g * _sigmoid(g) * u).astype(BF16)
        acc = acc + _dot(a, wd_ref[cols, :])
    o_ref[...] = x1 + mod_ref[0, 5:6, :] * acc


def _outffn(x, ym, yc, ya, mod, g2, wom, woc, woa, wg, wu, wd, *, seq, tm, mod_base, mod_stride):
    n, d = x.shape
    tpb = seq // tm
    tok = lambda c: pl.BlockSpec((tm, c), lambda i: (i, 0))
    const = lambda a: pl.BlockSpec(a.shape, lambda i: (0, 0), pipeline_mode=pl.Buffered(1))
    return pl.pallas_call(
        functools.partial(_outffn_kernel, ck=256),
        grid=(n // tm,),
        in_specs=[tok(d), tok(256), tok(256), tok(512),
                  pl.BlockSpec((1, 6, d), lambda i: (mod_base + mod_stride * (i // tpb), 0, 0)),
                  pl.BlockSpec((1, d), lambda i: (0, 0)),
                  const(wom), const(woc), const(woa), const(wg), const(wu), const(wd)],
        out_specs=tok(d),
        out_shape=jax.ShapeDtypeStruct((n, d), F32),
        compiler_params=_params("arbitrary"),
        name="outffn",
    )(x, ym, yc, ya, mod, g2, wom, woc, woa, wg, wu, wd)


def _rope_partner_cols(w):
    return jnp.concatenate([-w[..., 8:16], w[..., 0:8], -w[..., 24:32], w[..., 16:24]], axis=-1)


def _rope_partner_gain(g):
    return jnp.concatenate([g[8:16], g[0:8], g[24:32], g[16:24]])


def _place(x, lo, width=LANES):
    pad = [(0, 0)] * (x.ndim - 1) + [(lo, width - lo - x.shape[-1])]
    return jnp.pad(x, pad)


def _layer_weights(l, w_in, b_gates, mlstm_norm_g, conv_dw_w, conv_dw_b, conv_ln_g, conv_ln_b,
                   conv_pw_w, mla_cq_g, mla_w_uq, mla_ckv_g, mla_w_ukv, mla_qn_g, mla_kn_g,
                   w_out, norm1_g, norm2_g, ffn_w_gu, ffn_w_down):
    wi = w_in[l]
    d = wi.shape[0]
    q, k, v, o = wi[:, 0:256], wi[:, 256:512], wi[:, 512:768], wi[:, 768:1024]
    gates = wi[:, 1024:1040]
    a1, a2 = wi[:, 1040:1296], wi[:, 1296:1552]
    cq, ckv, kr = wi[:, 1552:1808], wi[:, 1808:1936], wi[:, 1936:1968]
    w_in_p = jnp.concatenate(
        [q * (MLSTM_DH ** -0.5), k, v, o, a1, a2, cq, ckv, _place(kr, MLA_NOPE),
         _place(_rope_partner_cols(kr), MLA_NOPE), _place(gates, 0)], axis=1).astype(BF16)
    dff = ffn_w_down.shape[1]
    wuq = mla_w_uq[l].reshape(-1, MLA_HEADS, MLA_QK).transpose(1, 0, 2)
    wukv = mla_w_ukv[l].reshape(-1, MLA_HEADS, MLA_NOPE + MLA_V).transpose(1, 0, 2)
    qn, kn = mla_qn_g[l], mla_kn_g[l]
    mla = {
        'ckv_g': mla_ckv_g[l][None, :],
        'cq_g': mla_cq_g[l][None, :],
        'wq': _place(wuq, 0).astype(BF16),
        'wqp': _place(_rope_partner_cols(wuq[..., MLA_NOPE:]), MLA_NOPE).astype(BF16),
        'wk': _place(wukv[..., :MLA_NOPE], 0).astype(BF16),
        'wv': _place(wukv[..., MLA_NOPE:], 0).astype(BF16),
        'gq': _place(qn, 0)[None, :],
        'gqp': _place(_rope_partner_gain(qn[MLA_NOPE:]), MLA_NOPE)[None, :],
        'gk': _place(kn, 0)[None, :],
        'gkp': _place(_rope_partner_gain(kn[MLA_NOPE:]), MLA_NOPE)[None, :],
    }
    wo = w_out[l].astype(BF16)
    return {
        'w_in': w_in_p,
        'b_gates': _place(b_gates[l], 0)[None, :],
        'norm1_g': norm1_g[l][None, :],
        'norm2_g': norm2_g[l][None, :],
        'mlstm_norm_g': mlstm_norm_g[l][None, :],
        'conv': (conv_dw_w[l], conv_dw_b[l][None, :], conv_ln_g[l][None, :], conv_ln_b[l][None, :],
                 conv_pw_w[l].astype(BF16)),
        'mla': mla,
        'wo_m': wo[0:256], 'wo_c': wo[256:512], 'wo_a': wo[512:1024],
        'wg': ffn_w_gu[l][:, :dff].astype(BF16), 'wu': ffn_w_gu[l][:, dff:].astype(BF16),
        'wd': ffn_w_down[l].astype(BF16),
    }


def _rope_tables(seq):
    t = jnp.arange(seq)
    quarter = MLA_ROPE // 4
    freqs = ROPE_BASE ** (-jnp.arange(quarter, dtype=F32) / quarter)
    ang_r = (t // GRID_W).astype(F32)[:, None] * freqs[None, :]
    ang_c = (t % GRID_W).astype(F32)[:, None] * freqs[None, :]
    ang = jnp.concatenate([ang_r, ang_r, ang_c, ang_c], axis=1)
    ones = jnp.ones((seq, MLA_NOPE), F32)
    cos = jnp.concatenate([ones, jnp.cos(ang), jnp.ones((seq, LANES - MLA_QK), F32)], axis=1)
    sin = _place(jnp.sin(ang), MLA_NOPE)
    return cos, sin


def _init_states(state_c, state_n, state_m):
    b = state_c.shape[0]
    dh = MLSTM_DH
    n_col = state_n[..., None]
    even = jnp.concatenate([state_c, n_col, jnp.zeros(state_c.shape[:-1] + (dh - 1,), F32)], axis=-1)
    odd = jnp.concatenate([n_col, jnp.zeros(state_c.shape[:-1] + (dh - 1,), F32), state_c], axis=-1)
    zeros = jnp.zeros_like(even)
    even = jnp.concatenate([even, zeros], axis=-2)
    odd = jnp.concatenate([zeros, odd], axis=-2)
    is_odd = (jnp.arange(MLSTM_HEADS) % 2 == 1)[None, None, :, None, None]
    c0 = jnp.where(is_odd, odd, even).reshape(b, 2 * MLSTM_HEADS, LANES, LANES)
    m0 = jnp.broadcast_to(state_m.reshape(b, 2 * MLSTM_HEADS, 1), (b, 2 * MLSTM_HEADS, LANES))
    return c0, m0


def _final_states(cf, mf):
    b = cf.shape[0]
    dh = MLSTM_DH
    cf = cf.reshape(b, 2, MLSTM_HEADS, LANES, LANES)
    c_even, n_even = cf[..., :dh, :dh], cf[..., :dh, dh]
    c_odd, n_odd = cf[..., dh:, dh:], cf[..., dh:, 0]
    is_odd = (jnp.arange(MLSTM_HEADS) % 2 == 1)
    c = jnp.where(is_odd[None, None, :, None, None], c_odd, c_even)
    n = jnp.where(is_odd[None, None, :, None], n_odd, n_even)
    m = mf[..., 0].reshape(b, 2, MLSTM_HEADS)
    return c, n, m


def _trunk_layer(x, mods, lw, *, batch, seq, tm, mod_base, mod_stride, ctx, rope_tabs):
    latent = ctx is not None
    (q, kt, v, o, u, cq, ckv_raw, kr, krp, gc, gr) = _inproj(
        x, mods, lw['norm1_g'], lw['w_in'], lw['b_gates'],
        seq=seq, tm=tm, mod_base=mod_base, mod_stride=mod_stride)
    c0, m0 = (ctx[2], ctx[3]) if latent else (None, None)
    res = _mlstm(q, kt, v, o, gc, gr, lw['mlstm_norm_g'], c0, m0,
                 batch=batch, seq=seq, write_final=not latent)
    y_m = res[0]
    y_c = _conv(u, *lw['conv'], batch=batch, seq=seq)
    cos, sin = rope_tabs if latent else (None, None)
    qh, kh, vh, ckvn = _mlaprep(ckv_raw, kr, krp, cq, cos, sin, lw['mla'], batch=batch, seq=seq,
                                tm=min(seq, 512), do_q=True, norm_ckv=True, rope=latent)
    if latent:
        past = ctx[0].shape[1]
        kc, vc, _ = _mlaprep(ctx[0].reshape(batch * past, -1),
                             _place(ctx[1], MLA_NOPE).reshape(batch * past, LANES),
                             None, None, None, None, lw['mla'], batch=batch, seq=past,
                             tm=past, do_q=False, norm_ckv=False, rope=False)
        y_a = _attn(qh, kh, vh, kc, vc, tq=256, tk=512)
    else:
        y_a = _attn(qh, kh, vh, None, None, tq=seq, tk=seq)
    y_a = y_a.reshape(batch * seq, -1)
    x = _outffn(x, y_m, y_c, y_a, mods, lw['norm2_g'], lw['wo_m'], lw['wo_c'], lw['wo_a'],
                lw['wg'], lw['wu'], lw['wd'], seq=seq, tm=tm, mod_base=mod_base,
                mod_stride=mod_stride)
    return x, ckvn, kr, res


def kernel(x_prompt, x_sample, cache_ckv, cache_krope, state_C, state_n, state_m, c, c_ctx, w_ada, b_ada, norm1_g, w_in, b_gates, mlstm_norm_g, conv_dw_w, conv_dw_b, conv_ln_g, conv_ln_b, conv_pw_w, mla_cq_g, mla_w_uq, mla_ckv_g, mla_w_ukv, mla_qn_g, mla_kn_g, w_out, norm2_g, ffn_w_gu, ffn_w_down):
    depth = w_in.shape[0]
    bsz, seq, d = x_prompt.shape
    dbsz, dseq, _ = x_sample.shape

    n_cond = 1 + dbsz
    rows = -(-n_cond // 8) * 8
    conds = jnp.concatenate([c_ctx[None, :], c, jnp.zeros((rows - n_cond, d), F32)], axis=0)
    mods = _ada(conds, w_ada, b_ada).reshape(depth, rows, 6, d)

    lws = [_layer_weights(l, w_in, b_gates, mlstm_norm_g, conv_dw_w, conv_dw_b, conv_ln_g,
                          conv_ln_b, conv_pw_w, mla_cq_g, mla_w_uq, mla_ckv_g, mla_w_ukv,
                          mla_qn_g, mla_kn_g, w_out, norm1_g, norm2_g, ffn_w_gu, ffn_w_down)
           for l in range(depth)]

    x = x_prompt.reshape(bsz * seq, d)
    ckv_l, kr_l, c_l, n_l, m_l = [], [], [], [], []
    for l in range(depth):
        x, ckvn, kr, res = _trunk_layer(x, mods[l], lws[l], batch=bsz, seq=seq, tm=seq,
                                        mod_base=0, mod_stride=0, ctx=None, rope_tabs=None)
        ckv_l.append(ckvn.reshape(bsz, seq, -1))
        kr_l.append(kr[:, MLA_NOPE:MLA_QK].reshape(bsz, seq, MLA_ROPE))
        cs, ns, msv = _final_states(res[1], res[2])
        c_l.append(cs); n_l.append(ns); m_l.append(msv)
    y_prompt = x.reshape(bsz, seq, d)

    rope_tabs = _rope_tables(dseq)
    x = x_sample.reshape(dbsz * dseq, d)
    for l in range(depth):
        c0, m0 = _init_states(state_C[:, l], state_n[:, l], state_m[:, l])
        ctx = (cache_ckv[:, l], cache_krope[:, l], c0, m0)
        x, _, _, _ = _trunk_layer(x, mods[l], lws[l], batch=dbsz, seq=dseq, tm=512,
                                  mod_base=1, mod_stride=1, ctx=ctx, rope_tabs=rope_tabs)
    y_sample = x.reshape(dbsz, dseq, d)

    return (y_prompt, y_sample, jnp.stack(ckv_l, axis=1), jnp.stack(kr_l, axis=1),
            jnp.stack(c_l, axis=1), jnp.stack(n_l, axis=1), jnp.stack(m_l, axis=1))
```
